```python
import math
import jax, jax.numpy as jnp
from jax import lax
import numpy as np

D_MODEL = 1024
BATCH = 2
SEQ = 16384
DEPTH = 1
DEC_BATCH = 1
DEC_SEQ = 16384
PAST_LEN = 128

MIX_WIDTH = D_MODEL
DIFF_WIDTH = MIX_WIDTH // 2
GLA_WIDTH = MIX_WIDTH - DIFF_WIDTH
N_HEADS_DIFF = 4
DIFF_QK_DIM = DIFF_WIDTH // (2 * N_HEADS_DIFF)
DIFF_V_DIM = 2 * DIFF_QK_DIM
ROPE_DIM = DIFF_QK_DIM // 4
ROPE_THETA = 500000.0
Q_BLOCK = 128
LAMBDA_STD = 0.1
N_HEADS_GLA = 4
GLA_K_DIM = GLA_WIDTH // (2 * N_HEADS_GLA)
GLA_V_DIM = GLA_WIDTH // N_HEADS_GLA
GLA_GATE_RANK = 16
GLA_GATE_TAU = 16.0
GLA_CHUNK = 64
N_GROUPS = 4
EXPERTS_PER_GROUP = 8
TOP_K_IN_GROUP = 2
D_EXPERT = 256
NORM_EPS = 1e-6
N_MOD = 6

SPLIT_SIZES = (
    N_HEADS_DIFF * 2 * DIFF_QK_DIM,
    N_HEADS_DIFF * 2 * DIFF_QK_DIM,
    N_HEADS_DIFF * DIFF_V_DIM,
    N_HEADS_GLA * GLA_K_DIM,
    N_HEADS_GLA * GLA_K_DIM,
    N_HEADS_GLA * GLA_V_DIM,
    N_HEADS_GLA * GLA_V_DIM,
    GLA_GATE_RANK,
    GLA_GATE_RANK,
)
IN_COLS = sum(SPLIT_SIZES)
SPLIT_POINTS = tuple(int(v) for v in np.cumsum(SPLIT_SIZES)[:-1])

kernel_name = "hymba_diffattn_gla_hiermoe_adaln_encoder"


def _rmsnorm(x, g):
    xf = x.astype(jnp.float32)
    y = xf * lax.rsqrt(jnp.mean(xf * xf, axis=-1, keepdims=True) + NORM_EPS)
    return (y * g.astype(jnp.float32)).astype(x.dtype)


def _modulate(h, shift, scale):
    return h * (1 + scale[:, None, :]) + shift[:, None, :]


def _rope_tables(seq_len):
    inv_freq = ROPE_THETA ** (-jnp.arange(0, ROPE_DIM, 2, dtype=jnp.float32) / ROPE_DIM)
    ang = jnp.arange(seq_len, dtype=jnp.float32)[:, None] * inv_freq[None, :]
    return jnp.cos(ang), jnp.sin(ang)


def _partial_rope(t, cos, sin):
    half = ROPE_DIM // 2
    c = cos[None, :, None, None, :]
    s = sin[None, :, None, None, :]
    t1 = t[..., :half]
    t2 = t[..., half:ROPE_DIM]
    rot = jnp.concatenate([t1 * c - t2 * s, t2 * c + t1 * s], axis=-1)
    return jnp.concatenate([rot, t[..., ROPE_DIM:]], axis=-1)


def _diff_attention(q, k, v, lam):
    B, S, H, _, dk = q.shape
    nb = S // Q_BLOCK
    qb = q.reshape(B, nb, Q_BLOCK, H, 2, dk).transpose(1, 0, 2, 3, 4, 5)
    scale = dk ** -0.5

    def block(qi):
        s = jnp.einsum('bqhcd,bkhcd->bhcqk', qi, k) * scale
        p = jax.nn.softmax(s, axis=-1)
        a = p[:, :, 0] - lam * p[:, :, 1]
        return jnp.einsum('bhqk,bkhe->bqhe', a, v)

    o = lax.map(block, qb)
    return o.transpose(1, 0, 2, 3, 4).reshape(B, S, H, -1)


def _gla_scan(q, k, v, log_a):
    B, S, H, dk = q.shape
    dv = v.shape[-1]
    C = GLA_CHUNK
    n = S // C

    def to_chunks(t):
        return t.reshape(B, n, C, H, t.shape[-1]).transpose(1, 0, 3, 2, 4)

    mask = jnp.tril(jnp.ones((C, C), dtype=bool))[:, :, None]

    def step(state, inp):
        qc, kc, vc, lac = inp
        b = jnp.cumsum(lac, axis=2)
        o_inter = jnp.einsum('bhcd,bhde->bhce', qc * jnp.exp(b), state)
        rel = b[:, :, :, None, :] - b[:, :, None, :, :]
        decay = jnp.exp(jnp.where(mask, rel, -jnp.inf))
        att = jnp.einsum('bhid,bhjd,bhijd->bhij', qc, kc, decay)
        o_intra = jnp.einsum('bhij,bhje->bhie', att, vc)
        b_last = b[:, :, -1:, :]
        new_state = jnp.exp(b_last[:, :, 0, :])[..., None] * state + jnp.einsum(
            'bhcd,bhce->bhde', kc * jnp.exp(b_last - b), vc)
        return new_state, o_inter + o_intra

    s0 = jnp.zeros((B, H, dk, dv), jnp.float32)
    _, o = lax.scan(step, s0, (to_chunks(q), to_chunks(k), to_chunks(v), to_chunks(log_a)))
    return o.transpose(1, 0, 3, 2, 4).reshape(B, S, H, dv)


def _token_mixer(h, cos, sin, lambda_init, w_in, lam_q1, lam_k1, lam_q2, lam_k2, diff_norm_g,
                 gla_wa_f, gla_ba_f, gla_wa_b, gla_ba_b, gla_norm_g, w_out):
    B, S, _ = h.shape
    f32 = jnp.float32
    proj = h @ w_in
    dq, dk_, dv_, gq, gk, gv, gg, zf, zb = jnp.split(proj, SPLIT_POINTS, axis=-1)

    q = _partial_rope(dq.reshape(B, S, N_HEADS_DIFF, 2, DIFF_QK_DIM).astype(f32), cos, sin)
    k = _partial_rope(dk_.reshape(B, S, N_HEADS_DIFF, 2, DIFF_QK_DIM).astype(f32), cos, sin)
    v = dv_.reshape(B, S, N_HEADS_DIFF, DIFF_V_DIM).astype(f32)
    lam = (jnp.exp(jnp.sum(lam_q1.astype(f32) * lam_k1.astype(f32)))
           - jnp.exp(jnp.sum(lam_q2.astype(f32) * lam_k2.astype(f32))) + lambda_init)
    o_d = _diff_attention(q, k, v, lam)
    o_d = _rmsnorm(o_d, diff_norm_g) * (1.0 - lambda_init)
    o_d = o_d.reshape(B, S, DIFF_WIDTH)

    qg = gq.reshape(B, S, N_HEADS_GLA, GLA_K_DIM).astype(f32) * (GLA_K_DIM ** -0.5)
    kg = gk.reshape(B, S, N_HEADS_GLA, GLA_K_DIM).astype(f32)
    vg = gv.reshape(B, S, N_HEADS_GLA, GLA_V_DIM).astype(f32)
    la_f = (jax.nn.log_sigmoid((zf @ gla_wa_f + gla_ba_f).astype(f32)) / GLA_GATE_TAU
            ).reshape(B, S, N_HEADS_GLA, GLA_K_DIM)
    la_b = (jax.nn.log_sigmoid((zb @ gla_wa_b + gla_ba_b).astype(f32)) / GLA_GATE_TAU
            ).reshape(B, S, N_HEADS_GLA, GLA_K_DIM)
    o_fwd = _gla_scan(qg, kg, vg, la_f)
    o_bwd = jnp.flip(_gla_scan(jnp.flip(qg, 1), jnp.flip(kg, 1), jnp.flip(vg, 1),
                               jnp.flip(la_b, 1)), axis=1)
    o_g = _rmsnorm(o_fwd + o_bwd, gla_norm_g) * jax.nn.silu(
        gg.reshape(B, S, N_HEADS_GLA, GLA_V_DIM).astype(f32))
    o_g = o_g.reshape(B, S, GLA_WIDTH)

    o = jnp.concatenate([o_d, o_g], axis=-1).astype(h.dtype)
    return o @ w_out


def _hier_moe(h, w_route_g, b_route_g, w_route_e, b_route_e, w_gate, w_up, w_down):
    B, S, D = h.shape
    f32 = jnp.float32
    hf = h.reshape(B * S, D)
    g_logits = (hf @ w_route_g + b_route_g).astype(f32)
    g_prob = jax.nn.softmax(g_logits, axis=-1)
    g_idx = jnp.argmax(g_logits, axis=-1)
    p_g = jnp.take_along_axis(g_prob, g_idx[:, None], axis=-1)
    e_logits = (hf @ w_route_e + b_route_e).astype(f32).reshape(-1, N_GROUPS, EXPERTS_PER_GROUP)
    e_in = jnp.take_along_axis(e_logits, g_idx[:, None, None], axis=1)[:, 0]
    top_v, top_i = lax.top_k(e_in, TOP_K_IN_GROUP)
    w_top = jax.nn.softmax(top_v, axis=-1) * p_g
    e_w = jnp.sum(jax.nn.one_hot(top_i, EXPERTS_PER_GROUP, dtype=f32) * w_top[..., None], axis=1)
    combine = (jax.nn.one_hot(g_idx, N_GROUPS, dtype=f32)[:, :, None] * e_w[:, None, :]
               ).astype(hf.dtype)
    out = jnp.zeros((B * S, D), f32)
    for g in range(N_GROUPS):
        a = jnp.einsum('td,edf->tef', hf, w_gate[g])
        u = jnp.einsum('td,edf->tef', hf, w_up[g])
        hid = jax.nn.silu(a) * u * combine[:, g, :, None]
        out = out + jnp.einsum('tef,efd->td', hid, w_down[g])
    return out.reshape(B, S, D).astype(h.dtype)


def _encode(x, c, norm1_g, norm2_g, w_ada, b_ada, w_in, lam_q1, lam_k1, lam_q2, lam_k2,
            diff_norm_g, gla_wa_f, gla_ba_f, gla_wa_b, gla_ba_b, gla_norm_g, w_out,
            w_route_g, b_route_g, w_route_e, b_route_e, w_exp_gate, w_exp_up, w_exp_down,
            final_norm_g, w_ada_f, b_ada_f):
    S = x.shape[1]
    cos, sin = _rope_tables(S)
    sc = jax.nn.silu(c)
    for l in range(DEPTH):
        lambda_init = 0.8 - 0.6 * math.exp(-0.3 * l)
        mod = (sc @ w_ada[l] + b_ada[l]).astype(x.dtype)
        sh1, sc1, g1, sh2, sc2, g2 = jnp.split(mod, N_MOD, axis=-1)
        h = _modulate(_rmsnorm(x, norm1_g[l]), sh1, sc1)
        x = x + g1[:, None, :] * _token_mixer(
            h, cos, sin, lambda_init, w_in[l], lam_q1[l], lam_k1[l], lam_q2[l], lam_k2[l],
            diff_norm_g[l], gla_wa_f[l], gla_ba_f[l], gla_wa_b[l], gla_ba_b[l], gla_norm_g[l],
            w_out[l])
        h = _modulate(_rmsnorm(x, norm2_g[l]), sh2, sc2)
        x = x + g2[:, None, :] * _hier_moe(
            h, w_route_g[l], b_route_g[l], w_route_e[l], b_route_e[l],
            w_exp_gate[l], w_exp_up[l], w_exp_down[l])
    mod_f = (sc @ w_ada_f + b_ada_f).astype(x.dtype)
    sh_f, sc_f = jnp.split(mod_f, 2, axis=-1)
    return _modulate(_rmsnorm(x, final_norm_g), sh_f, sc_f)


def setup_inputs(seed: int = 0) -> dict:
    key = jax.random.key(seed)
    ks = jax.random.split(key, 40)
    f32 = jnp.float32
    D, L = D_MODEL, DEPTH
    nrm = lambda k, shape, s: jax.random.normal(k, shape, f32) * s
    GE = N_GROUPS * EXPERTS_PER_GROUP
    return {
        "x_prompt": nrm(ks[0], (BATCH, SEQ, D), 1.0),
        "x_sample": nrm(ks[1], (DEC_BATCH, DEC_SEQ, D), 1.0),
        "c_prompt": nrm(ks[2], (BATCH, D), 1.0),
        "c_sample": nrm(ks[3], (DEC_BATCH, D), 1.0),
        "norm1_g": 1.0 + nrm(ks[4], (L, D), 0.02),
        "norm2_g": 1.0 + nrm(ks[5], (L, D), 0.02),
        "w_ada": nrm(ks[6], (L, D, N_MOD * D), 0.5 * D ** -0.5),
        "b_ada": nrm(ks[7], (L, N_MOD * D), 0.02),
        "w_in": nrm(ks[8], (L, D, IN_COLS), D ** -0.5),
        "lam_q1": nrm(ks[9], (L, DIFF_QK_DIM), LAMBDA_STD),
        "lam_k1": nrm(ks[10], (L, DIFF_QK_DIM), LAMBDA_STD),
        "lam_q2": nrm(ks[11], (L, DIFF_QK_DIM), LAMBDA_STD),
        "lam_k2": nrm(ks[12], (L, DIFF_QK_DIM), LAMBDA_STD),
        "diff_norm_g": 1.0 + nrm(ks[13], (L, DIFF_V_DIM), 0.02),
        "gla_wa_f": nrm(ks[14], (L, GLA_GATE_RANK, N_HEADS_GLA * GLA_K_DIM), GLA_GATE_RANK ** -0.5),
        "gla_ba_f": nrm(ks[15], (L, N_HEADS_GLA * GLA_K_DIM), 0.1),
        "gla_wa_b": nrm(ks[16], (L, GLA_GATE_RANK, N_HEADS_GLA * GLA_K_DIM), GLA_GATE_RANK ** -0.5),
        "gla_ba_b": nrm(ks[17], (L, N_HEADS_GLA * GLA_K_DIM), 0.1),
        "gla_norm_g": 1.0 + nrm(ks[18], (L, GLA_V_DIM), 0.02),
        "w_out": nrm(ks[19], (L, MIX_WIDTH, D), MIX_WIDTH ** -0.5),
        "w_route_g": nrm(ks[20], (L, D, N_GROUPS), D ** -0.5),
        "b_route_g": nrm(ks[21], (L, N_GROUPS), 0.01),
        "w_route_e": nrm(ks[22], (L, D, GE), D ** -0.5),
        "b_route_e": nrm(ks[23], (L, GE), 0.01),
        "w_exp_gate": nrm(ks[24], (L, N_GROUPS, EXPERTS_PER_GROUP, D, D_EXPERT), D ** -0.5),
        "w_exp_up": nrm(ks[25], (L, N_GROUPS, EXPERTS_PER_GROUP, D, D_EXPERT), D ** -0.5),
        "w_exp_down": nrm(ks[26], (L, N_GROUPS, EXPERTS_PER_GROUP, D_EXPERT, D), D_EXPERT ** -0.5),
        "final_norm_g": 1.0 + nrm(ks[27], (D,), 0.02),
        "w_ada_f": nrm(ks[28], (D, 2 * D), 0.5 * D ** -0.5),
        "b_ada_f": nrm(ks[29], (2 * D,), 0.02),
    }


def reference(x_prompt, x_sample, c_prompt, c_sample, norm1_g, norm2_g, w_ada, b_ada, w_in,
              lam_q1, lam_k1, lam_q2, lam_k2, diff_norm_g, gla_wa_f, gla_ba_f, gla_wa_b, gla_ba_b,
              gla_norm_g, w_out, w_route_g, b_route_g, w_route_e, b_route_e, w_exp_gate, w_exp_up,
              w_exp_down, final_norm_g, w_ada_f, b_ada_f):
    y_prompt = _encode(x_prompt, c_prompt, norm1_g, norm2_g, w_ada, b_ada, w_in, lam_q1, lam_k1,
                       lam_q2, lam_k2, diff_norm_g, gla_wa_f, gla_ba_f, gla_wa_b, gla_ba_b,
                       gla_norm_g, w_out, w_route_g, b_route_g, w_route_e, b_route_e,
                       w_exp_gate, w_exp_up, w_exp_down, final_norm_g, w_ada_f, b_ada_f)
    y_sample = _encode(x_sample, c_sample, norm1_g, norm2_g, w_ada, b_ada, w_in, lam_q1, lam_k1,
                       lam_q2, lam_k2, diff_norm_g, gla_wa_f, gla_ba_f, gla_wa_b, gla_ba_b,
                       gla_norm_g, w_out, w_route_g, b_route_g, w_route_e, b_route_e,
                       w_exp_gate, w_exp_up, w_exp_down, final_norm_g, w_ada_f, b_ada_f)
    return (y_prompt, y_sample)
```

```python
import functools
import math

import numpy as np
import jax
import jax.numpy as jnp
from jax import lax
from jax.experimental import pallas as pl
from jax.experimental.pallas import tpu as pltpu

F32 = jnp.float32
BF16 = jnp.bfloat16

D_MODEL = 1024
N_HEADS_DIFF = 4
DIFF_QK_DIM = 64
DIFF_V_DIM = 128
ROPE_DIM = 16
ROPE_THETA = 500000.0
N_HEADS_GLA = 4
GLA_K_DIM = 64
GLA_V_DIM = 128
GLA_GATE_RANK = 16
GLA_GATE_TAU = 16.0
GLA_CHUNK = 64
GLA_SUB = 16
N_GROUPS = 4
EXPERTS_PER_GROUP = 8
D_EXPERT = 256
NORM_EPS = 1e-6
N_MOD = 6
LAMBDA_INIT = 0.8 - 0.6 * math.exp(-0.3 * 0)
LOG2E = 1.4426950408889634

LANES = 128
VMEM_LIMIT = 56 * 1024 * 1024

_C_DQ, _C_DK, _C_DV = 0, 512, 1024
_C_GQ, _C_GK, _C_GV, _C_GG, _C_Z = 1536, 1792, 2048, 2560, 3072
_C_MAIN = _C_Z


def _dot(a, b):
    return jnp.dot(a, b, preferred_element_type=F32)


def _dot_nt(a, b):
    return lax.dot_general(a, b, (((1,), (1,)), ((), ())), preferred_element_type=F32)


def _dot_tn(a, b):
    return lax.dot_general(a, b, (((0,), (0,)), ((), ())), preferred_element_type=F32)


def _split2(x):
    hi = x.astype(BF16)
    lo = (x - hi.astype(F32)).astype(BF16)
    return hi, lo


def _dot3(a_hi, a_lo, b_hi, b_lo):
    return _dot(a_hi, b_hi) + (_dot(a_lo, b_hi) + _dot(a_hi, b_lo))


def _silu(x):
    return x / (1.0 + jnp.exp(-x))


def _log_sigmoid(x):
    return jnp.minimum(x, 0.0) - jnp.log(1.0 + jnp.exp(-jnp.abs(x)))


def _rms(x, g):
    ms = jnp.mean(x * x, axis=-1, keepdims=True)
    return x * lax.rsqrt(ms + NORM_EPS) * g


def _lane_tile(x, n):
    return jnp.concatenate([x] * n, axis=1)


def _params(sem, vmem=VMEM_LIMIT):
    return pltpu.CompilerParams(dimension_semantics=sem, vmem_limit_bytes=vmem)


def _ada_kernel(c_ref, w_ref, b_ref, o_ref):
    a_hi, a_lo = _split2(_silu(c_ref[...]))
    w_hi, w_lo = _split2(w_ref[...])
    o_ref[...] = _dot3(a_hi, a_lo, w_hi, w_lo) + b_ref[...]


def _ada(c8, w, b):
    d, n = w.shape
    tn = 1024
    return pl.pallas_call(
        _ada_kernel,
        out_shape=jax.ShapeDtypeStruct((c8.shape[0], n), F32),
        grid=(n // tn,),
        in_specs=[pl.BlockSpec((c8.shape[0], d), lambda j: (0, 0)),
                  pl.BlockSpec((d, tn), lambda j: (0, j)),
                  pl.BlockSpec((1, tn), lambda j: (0, j))],
        out_specs=pl.BlockSpec((c8.shape[0], tn), lambda j: (0, j)),
        compiler_params=_params(("arbitrary",)),
        name="ada",
    )(c8, w, b.reshape(1, n))


def _pre_kernel(x_ref, g_ref, sh_ref, sc_ref, wm_ref, wz_ref, wgh_ref, wgl_ref, bg_ref,
                rc_ref, rs1_ref, rs2_ref,
                qd_ref, kd_ref, vd_ref, gq_ref, gk_ref, gv_ref, gg_ref, la_ref):
    x = x_ref[...]
    h = _rms(x, g_ref[...]) * (1.0 + sc_ref[...]) + sh_ref[...]
    hb = h.astype(BF16)
    proj = _dot(hb, wm_ref[...])
    z = _dot(hb, wz_ref[...])

    rc, rs1, rs2 = rc_ref[...], rs1_ref[...], rs2_ref[...]

    def rope(t):
        return (t * rc + pltpu.roll(t, LANES - ROPE_DIM // 2, 1) * rs1
                + pltpu.roll(t, ROPE_DIM // 2, 1) * rs2)

    q_scale = (DIFF_QK_DIM ** -0.5) * LOG2E
    for hd in range(N_HEADS_DIFF):
        lo = hd * LANES
        qd_ref[:, lo:lo + LANES] = (rope(proj[:, _C_DQ + lo:_C_DQ + lo + LANES]) * q_scale).astype(BF16)
        kd_ref[:, lo:lo + LANES] = rope(proj[:, _C_DK + lo:_C_DK + lo + LANES]).astype(BF16)
    vd_ref[...] = proj[:, _C_DV:_C_GQ].astype(BF16)
    gq_ref[...] = (proj[:, _C_GQ:_C_GK] * (GLA_K_DIM ** -0.5)).astype(BF16)
    gk_ref[...] = proj[:, _C_GK:_C_GV].astype(BF16)
    gv_ref[...] = proj[:, _C_GV:_C_GG].astype(BF16)
    gg_ref[...] = proj[:, _C_GG:_C_Z].astype(BF16)

    z_hi, z_lo = _split2(z)
    zl = _dot3(z_hi, z_lo, wgh_ref[...], wgl_ref[...]) + bg_ref[...]
    la_ref[...] = _log_sigmoid(zl) * (1.0 / GLA_GATE_TAU)


def _rope_tables(seq_len):
    half = ROPE_DIM // 2
    inv_freq = ROPE_THETA ** (-jnp.arange(0, ROPE_DIM, 2, dtype=F32) / ROPE_DIM)
    ang = jnp.arange(seq_len, dtype=F32)[:, None] * inv_freq[None, :]
    cos, sin = jnp.cos(ang), jnp.sin(ang)
    ones = jnp.ones((seq_len, DIFF_QK_DIM - ROPE_DIM), F32)
    zeros = jnp.zeros((seq_len, DIFF_QK_DIM - ROPE_DIM), F32)
    zh = jnp.zeros((seq_len, half), F32)
    c64 = jnp.concatenate([cos, cos, ones], axis=1)
    s1_64 = jnp.concatenate([-sin, zh, zeros], axis=1)
    s2_64 = jnp.concatenate([zh, sin, zeros], axis=1)
    rep = lambda t: jnp.concatenate([t, t], axis=1)
    return rep(c64), rep(s1_64), rep(s2_64)


def _pre(x, g1, sh1, sc1, w_main, w_z, wg_hi, wg_lo, b_gate, tm):
    nb, s, d = x.shape
    rc, rs1, rs2 = _rope_tables(s)
    row = lambda w: pl.BlockSpec((None, tm, w), lambda b, i: (b, i, 0))
    full = lambda a: pl.BlockSpec(a.shape, lambda b, i: (0,) * a.ndim)
    per_seq = pl.BlockSpec((None, 1, d), lambda b, i: (b, 0, 0))
    tab = pl.BlockSpec((tm, LANES), lambda b, i: (i, 0))
    out_w = (512, 512, 512, 256, 256, 512, 512, 512)
    out_dt = (BF16,) * 7 + (F32,)
    return pl.pallas_call(
        _pre_kernel,
        out_shape=[jax.ShapeDtypeStruct((nb, s, w), dt) for w, dt in zip(out_w, out_dt)],
        grid=(nb, s // tm),
        in_specs=[row(d), full(g1), per_seq, per_seq, full(w_main), full(w_z), full(wg_hi),
                  full(wg_lo), full(b_gate), tab, tab, tab],
        out_specs=[row(w) for w in out_w],
        compiler_params=_params(("arbitrary", "arbitrary")),
        name="pre",
    )(x, g1, sh1, sc1, w_main, w_z, wg_hi, wg_lo, b_gate, rc, rs1, rs2)


def _attn_kernel(lam_ref, gn_ref, q_ref, k_ref, v_ref, o_ref, vp_scr, acc_scr, m_scr, *, tk):
    tq = q_ref.shape[0]
    s_len = k_ref.shape[0]

    @pl.when(pl.program_id(2) == 0)
    def _():
        vp_scr[:, :LANES] = v_ref[...]
        vp_scr[:, LANES:] = jnp.ones((s_len, LANES), BF16)

    q = q_ref[...]
    lane = lax.broadcasted_iota(jnp.int32, q.shape, 1)
    zero = jnp.zeros_like(q)
    qs = (jnp.where(lane < DIFF_QK_DIM, q, zero), jnp.where(lane >= DIFF_QK_DIM, q, zero))

    acc_scr[...] = jnp.zeros_like(acc_scr)
    m_scr[...] = jnp.full_like(m_scr, -jnp.inf)

    def body(j, carry):
        off = pl.multiple_of(j * tk, tk)
        k = k_ref[pl.ds(off, tk), :]
        vp = vp_scr[pl.ds(off, tk), :]
        for c in range(2):
            rows = slice(c * tq, (c + 1) * tq)
            s = _dot_nt(qs[c], k)
            m_prev = m_scr[rows, :]
            m_new = jnp.maximum(m_prev, jnp.max(s, axis=1, keepdims=True))
            p = jnp.exp2(s - _lane_tile(m_new, tk // LANES))
            alpha = jnp.exp2(m_prev - m_new)
            pv = _dot(p.astype(BF16), vp)
            acc_scr[rows, :] = acc_scr[rows, :] * _lane_tile(alpha, 2) + pv
            m_scr[rows, :] = m_new
        return carry

    lax.fori_loop(0, s_len // tk, body, 0)

    lv = lam_ref[...]
    lam = (jnp.exp(jnp.sum(lv[0:1] * lv[1:2], axis=-1, keepdims=True))
           - jnp.exp(jnp.sum(lv[2:3] * lv[3:4], axis=-1, keepdims=True)) + LAMBDA_INIT)
    a0 = acc_scr[0:tq, :]
    a1 = acc_scr[tq:2 * tq, :]
    o = a0[:, :LANES] / a0[:, LANES:] - lam * (a1[:, :LANES] / a1[:, LANES:])
    o_ref[...] = (_rms(o, gn_ref[...]) * (1.0 - LAMBDA_INIT)).astype(BF16)


def _attn(lamv, gn, qd, kd, vd, tq, tk):
    nb, s, _ = qd.shape
    return pl.pallas_call(
        functools.partial(_attn_kernel, tk=tk),
        out_shape=jax.ShapeDtypeStruct((nb, s, N_HEADS_DIFF * DIFF_V_DIM), BF16),
        grid=(nb, N_HEADS_DIFF, s // tq),
        in_specs=[pl.BlockSpec(lamv.shape, lambda b, h, i: (0, 0)),
                  pl.BlockSpec(gn.shape, lambda b, h, i: (0, 0)),
                  pl.BlockSpec((None, tq, LANES), lambda b, h, i: (b, i, h)),
                  pl.BlockSpec((None, s, LANES), lambda b, h, i: (b, 0, h)),
                  pl.BlockSpec((None, s, LANES), lambda b, h, i: (b, 0, h))],
        out_specs=pl.BlockSpec((None, tq, LANES), lambda b, h, i: (b, i, h)),
        scratch_shapes=[pltpu.VMEM((s, 2 * LANES), BF16),
                        pltpu.VMEM((2 * tq, 2 * LANES), F32),
                        pltpu.VMEM((2 * tq, LANES), F32)],
        compiler_params=_params(("arbitrary", "arbitrary", "arbitrary")),
        name="attn",
    )(lamv, gn, qd, kd, vd)


_C = GLA_CHUNK
_NSUB = GLA_CHUNK // GLA_SUB
_G_ROWS = (3 + _NSUB) * _C


def _gla_constants(reverse):
    i = np.arange(_C)[:, None]
    u = np.arange(_C)[None, :]
    if not reverse:
        cum = u <= i
        r_blk = [np.broadcast_to(u < GLA_SUB * I, (_C, _C)) for I in range(_NSUB)]
        rq = u < GLA_SUB * (i // GLA_SUB)
        causal = u <= i
        valid = [np.broadcast_to(i < GLA_SUB * (I + 1), (_C, LANES)) for I in range(_NSUB)]
    else:
        cum = u >= i
        r_blk = [np.broadcast_to(u >= GLA_SUB * (I + 1), (_C, _C)) for I in range(_NSUB)]
        rq = u >= GLA_SUB * (i // GLA_SUB + 1)
        causal = u >= i
        valid = [np.broadcast_to(i >= GLA_SUB * I, (_C, LANES)) for I in range(_NSUB)]
    cmat = np.concatenate([cum] + r_blk + [np.ones((_C, _C), bool), rq], axis=0)
    cm = np.zeros((2 * _C, LANES), np.float32)
    cm[:_C, :_C] = causal
    cm[_C:, :_C] = causal
    kvalid = np.concatenate(valid, axis=1).astype(np.float32)
    lane = np.arange(LANES)[None, :]
    qsel = []
    for hd in range(2):
        head_lanes = (lane // GLA_K_DIM) == hd
        qsel.append(np.concatenate(
            [np.broadcast_to(((i // GLA_SUB) == I) & head_lanes, (_C, LANES)) for I in range(_NSUB)],
            axis=1))
    qsel = np.concatenate(qsel, axis=0).astype(np.float32)
    return (jnp.asarray(cmat, BF16), jnp.asarray(cm), jnp.asarray(kvalid), jnp.asarray(qsel))


def _gla_chunk(la, q, k, v, state, cmat, causal, kvalid, qsel):
    la_hi, la_lo = _split2(la)
    lacat = jnp.concatenate([la_hi, la_lo], axis=1)
    g = _dot(cmat, lacat)
    gs = g[:, :LANES] + g[:, LANES:]
    b = gs[0:_C]
    r_blk = [gs[(1 + I) * _C:(2 + I) * _C] for I in range(_NSUB)]
    tot = gs[(1 + _NSUB) * _C:(2 + _NSUB) * _C]
    rq = gs[(2 + _NSUB) * _C:(3 + _NSUB) * _C]

    qf = q.astype(F32)
    kf = k.astype(F32)
    q_sub = qf * jnp.exp(b - rq)
    q_st = qf * jnp.exp(b)
    k_st = (kf * jnp.exp(tot - b)).astype(BF16)
    zero = jnp.zeros_like(kf)
    khat = jnp.concatenate(
        [jnp.where(kvalid[:, I * LANES:(I + 1) * LANES] > 0, kf * jnp.exp(r_blk[I] - b), zero)
         for I in range(_NSUB)], axis=1).astype(BF16)
    khat = jnp.concatenate([khat, jnp.zeros_like(khat)], axis=0)

    q_sub4 = jnp.concatenate([q_sub] * _NSUB, axis=1)
    q_sub8 = jnp.concatenate([q_sub4, q_sub4], axis=0)
    qhat = jnp.where(qsel > 0, q_sub8, jnp.zeros_like(q_sub8)).astype(BF16)
    att = _dot_nt(qhat, khat)
    att = jnp.where(causal > 0, att, jnp.zeros_like(att))

    lane = lax.broadcasted_iota(jnp.int32, q_st.shape, 1)
    zq = jnp.zeros_like(q_st)
    q_st2 = jnp.concatenate([jnp.where(lane < GLA_K_DIM, q_st, zq),
                             jnp.where(lane >= GLA_K_DIM, q_st, zq)], axis=0)
    lhs = jnp.concatenate([q_st2, att], axis=1).astype(BF16)
    rhs = jnp.concatenate([state.astype(BF16), v, jnp.zeros_like(v)], axis=0)
    o_all = _dot(lhs, rhs)
    o = jnp.concatenate([o_all[0:_C, 0:LANES], o_all[_C:2 * _C, LANES:2 * LANES]], axis=1)

    kv = _dot_tn(k_st, v)
    dcol = _dot_tn(lacat, jnp.ones((_C, LANES), BF16))
    decay = jnp.exp(dcol[:LANES] + dcol[LANES:])
    new_state = state * jnp.concatenate([decay, decay], axis=1) + kv
    return o, new_state


def _gla_kernel(cf_ref, mf_ref, kvf_ref, qsf_ref, cb_ref, mb_ref, kvb_ref, qsb_ref,
                qf_ref, kf_ref, vf_ref, laf_ref, qb_ref, kb_ref, vb_ref, lab_ref,
                of_ref, ob_ref, st_scr):
    @pl.when(pl.program_id(1) == 0)
    def _():
        st_scr[...] = jnp.zeros_like(st_scr)

    n_chunks = qf_ref.shape[0] // _C
    consts = ((cf_ref[...], mf_ref[...], kvf_ref[...], qsf_ref[...]),
              (cb_ref[...], mb_ref[...], kvb_ref[...], qsb_ref[...]))
    refs = ((qf_ref, kf_ref, vf_ref, laf_ref, of_ref), (qb_ref, kb_ref, vb_ref, lab_ref, ob_ref))
    for c in range(n_chunks):
        for d in range(2):
            q_ref, k_ref, v_ref, la_ref, o_ref = refs[d]
            cc = c if d == 0 else n_chunks - 1 - c
            rows = slice(cc * _C, (cc + 1) * _C)
            for pr in range(N_HEADS_GLA // 2):
                kl = slice(pr * LANES, (pr + 1) * LANES)
                vl = slice(pr * 2 * LANES, (pr + 1) * 2 * LANES)
                o, st = _gla_chunk(la_ref[rows, kl], q_ref[rows, kl], k_ref[rows, kl],
                                   v_ref[rows, vl], st_scr[d, pr], *consts[d])
                st_scr[d, pr] = st
                o_ref[rows, vl] = o


def _gla(gq, gk, gv, la, lc):
    nb, s, _ = gq.shape
    nblk = s // lc
    cf = _gla_constants(False)
    cb = _gla_constants(True)
    full = lambda a: pl.BlockSpec(a.shape, lambda b, i: (0,) * a.ndim)
    fwd = lambda w, col=0: pl.BlockSpec((None, lc, w), lambda b, i: (b, i, col))
    bwd = lambda w, col=0: pl.BlockSpec((None, lc, w), lambda b, i: (b, nblk - 1 - i, col))
    kw, vw = N_HEADS_GLA * GLA_K_DIM, N_HEADS_GLA * GLA_V_DIM
    return pl.pallas_call(
        _gla_kernel,
        out_shape=[jax.ShapeDtypeStruct((nb, s, vw), F32)] * 2,
        grid=(nb, nblk),
        in_specs=[full(a) for a in cf] + [full(a) for a in cb]
        + [fwd(kw), fwd(kw), fwd(vw), fwd(kw, 0), bwd(kw), bwd(kw), bwd(vw), bwd(kw, 1)],
        out_specs=[fwd(vw), bwd(vw)],
        scratch_shapes=[pltpu.VMEM((2, N_HEADS_GLA // 2, LANES, 2 * LANES), F32)],
        compiler_params=_params(("arbitrary", "arbitrary")),
        name="gla",
    )(*cf, *cb, gq, gk, gv, la, gq, gk, gv, la)


def _post_kernel(x_ref, od_ref, of_ref, ob_ref, gg_ref, gng_ref, wo_ref, g1_ref, n2_ref,
                 sh2_ref, sc2_ref, wrh_ref, wrl_ref, br_ref,
                 x1_ref, h2_ref, route_ref, cnt_ref, run_scr):
    first = jnp.logical_and(pl.program_id(0) == 0, pl.program_id(1) == 0)

    @pl.when(first)
    def _():
        run_scr[...] = jnp.zeros_like(run_scr)

    tm = x_ref.shape[0]
    og = of_ref[...] + ob_ref[...]
    gng = gng_ref[...]
    half = N_HEADS_DIFF * DIFF_V_DIM
    mix = _dot(od_ref[...], wo_ref[0:half, :])
    parts = []
    for hd in range(N_HEADS_GLA):
        sl = slice(hd * LANES, (hd + 1) * LANES)
        parts.append((_rms(og[:, sl], gng) * _silu(gg_ref[:, sl].astype(F32))).astype(BF16))
    mix = mix + _dot(jnp.concatenate(parts, axis=1), wo_ref[half:, :])
    x1 = x_ref[...] + g1_ref[...] * mix
    x1_ref[...] = x1
    h2 = _rms(x1, n2_ref[...]) * (1.0 + sc2_ref[...]) + sh2_ref[...]
    h2_ref[...] = h2

    h_hi, h_lo = _split2(h2)
    logits = _dot3(h_hi, h_lo, wrh_ref[...], wrl_ref[...]) + br_ref[...]
    lane = lax.broadcasted_iota(jnp.int32, logits.shape, 1)
    mx = jnp.max(logits, axis=-1, keepdims=True)
    idx = jnp.min(jnp.where(logits == mx, lane, LANES), axis=-1, keepdims=True)
    onehot = (lane == idx).astype(F32)
    r_i = lax.broadcasted_iota(jnp.int32, (tm, tm), 0)
    c_i = lax.broadcasted_iota(jnp.int32, (tm, tm), 1)
    before = (c_i < r_i).astype(BF16)
    prior = _dot(before, onehot.astype(BF16)) + run_scr[...]
    rank = jnp.sum(onehot * prior, axis=-1, keepdims=True)
    run_scr[...] = run_scr[...] + jnp.sum(onehot, axis=0, keepdims=True)
    route_ref[...] = jnp.where(lane == 0, idx.astype(F32), jnp.where(lane == 1, rank, 0.0))
    cnt_ref[...] = run_scr[...]


def _post(x, od, of, ob, gg, gng, wo, g1, n2, sh2, sc2, wr_hi, wr_lo, br, tm):
    nb, s, d = x.shape
    row = lambda w: pl.BlockSpec((None, tm, w), lambda b, i: (b, i, 0))
    full = lambda a: pl.BlockSpec(a.shape, lambda b, i: (0,) * a.ndim)
    per_seq = pl.BlockSpec((None, 1, d), lambda b, i: (b, 0, 0))
    return pl.pallas_call(
        _post_kernel,
        out_shape=[jax.ShapeDtypeStruct((nb, s, d), F32), jax.ShapeDtypeStruct((nb, s, d), F32),
                   jax.ShapeDtypeStruct((nb, s, LANES), F32), jax.ShapeDtypeStruct((1, LANES), F32)],
        grid=(nb, s // tm),
        in_specs=[row(d), row(512), row(512), row(512), row(512), full(gng), full(wo), per_seq,
                  full(n2), per_seq, per_seq, full(wr_hi), full(wr_lo), full(br)],
        out_specs=[row(d), row(d), row(LANES), pl.BlockSpec((1, LANES), lambda b, i: (0, 0))],
        scratch_shapes=[pltpu.VMEM((1, LANES), F32)],
        compiler_params=_params(("arbitrary", "arbitrary")),
        name="post",
    )(x, od, of, ob, gg, gng, wo, g1, n2, sh2, sc2, wr_hi, wr_lo, br)


def _row_copy(src_ref, src_row, dst_ref, dst_row, sem):
    return pltpu.make_async_copy(src_ref.at[pl.ds(src_row, 1)], dst_ref.at[pl.ds(dst_row, 1)], sem)


def _scatter_kernel(pos_ref, h_ref, init_ref, xs_ref, sem):
    del init_ref
    tm = h_ref.shape[0]

    def start(r, c):
        _row_copy(h_ref, r, xs_ref, pos_ref[0, r], sem).start()
        return c

    def wait(r, c):
        _row_copy(h_ref, 0, xs_ref, 0, sem).wait()
        return c

    lax.fori_loop(0, tm, start, 0)
    lax.fori_loop(0, tm, wait, 0)


def _scatter(pos3, h2, n_rows, tm):
    t, d = h2.shape
    init = jnp.zeros((n_rows, d), F32)
    return pl.pallas_call(
        _scatter_kernel,
        out_shape=jax.ShapeDtypeStruct((n_rows, d), F32),
        grid=(t // tm,),
        in_specs=[pl.BlockSpec((None, 1, tm), lambda i: (i, 0, 0), memory_space=pltpu.SMEM),
                  pl.BlockSpec((tm, d), lambda i: (i, 0)),
                  pl.BlockSpec(memory_space=pl.ANY)],
        out_specs=pl.BlockSpec(memory_space=pl.ANY),
        scratch_shapes=[pltpu.SemaphoreType.DMA(())],
        input_output_aliases={2: 0},
        compiler_params=_params(("arbitrary",)),
        name="scatter",
    )(pos3, h2, init)


def _final_kernel(pos_ref, ys_ref, x1_ref, g2_ref, fn_ref, shf_ref, scf_ref, o_ref, buf, sem):
    tm = x1_ref.shape[0]

    def start(r, c):
        _row_copy(ys_ref, pos_ref[0, r], buf, r, sem).start()
        return c

    def wait(r, c):
        _row_copy(ys_ref, 0, buf, 0, sem).wait()
        return c

    lax.fori_loop(0, tm, start, 0)
    lax.fori_loop(0, tm, wait, 0)
    y = x1_ref[...] + g2_ref[...] * buf[...]
    o_ref[...] = _rms(y, fn_ref[...]) * (1.0 + scf_ref[...]) + shf_ref[...]


def _final(pos3, ys, x1, g2, fn, shf, scf, tm):
    nb, s, d = x1.shape
    nt = s // tm
    row = pl.BlockSpec((None, tm, d), lambda b, i: (b, i, 0))
    per_seq = pl.BlockSpec((None, 1, d), lambda b, i: (b, 0, 0))
    return pl.pallas_call(
        _final_kernel,
        out_shape=jax.ShapeDtypeStruct((nb, s, d), F32),
        grid=(nb, nt),
        in_specs=[pl.BlockSpec((None, 1, tm), lambda b, i: (b * nt + i, 0, 0), memory_space=pltpu.SMEM),
                  pl.BlockSpec(memory_space=pl.ANY), row, per_seq,
                  pl.BlockSpec(fn.shape, lambda b, i: (0, 0)), per_seq, per_seq],
        out_specs=row,
        scratch_shapes=[pltpu.VMEM((tm, d), F32), pltpu.SemaphoreType.DMA(())],
        compiler_params=_params(("arbitrary", "arbitrary")),
        name="final",
    )(pos3, ys, x1, g2, fn, shf, scf)


_GL0 = EXPERTS_PER_GROUP


def _moe_kernel(tg_ref, x_ref, wrh_ref, wrl_ref, br_ref, wg_ref, wu_ref, wd_ref, o_ref):
    grp = tg_ref[pl.program_id(0)]
    x = x_ref[...]
    x_hi, x_lo = _split2(x)
    logits = _dot3(x_hi, x_lo, wrh_ref[...], wrl_ref[...]) + br_ref[...]
    lane = lax.broadcasted_iota(jnp.int32, logits.shape, 1)
    ninf = jnp.full_like(logits, -jnp.inf)

    gl = jnp.where(jnp.logical_and(lane >= _GL0, lane < _GL0 + N_GROUPS), logits, ninf)
    ge = jnp.exp(gl - jnp.max(gl, axis=-1, keepdims=True))
    p_g = (jnp.sum(jnp.where(lane == _GL0 + grp, ge, 0.0), axis=-1, keepdims=True)
           / jnp.sum(ge, axis=-1, keepdims=True))

    el = jnp.where(lane < EXPERTS_PER_GROUP, logits, ninf)
    m1 = jnp.max(el, axis=-1, keepdims=True)
    i1 = jnp.min(jnp.where(el == m1, lane, LANES), axis=-1, keepdims=True)
    el2 = jnp.where(lane == i1, ninf, el)
    m2 = jnp.max(el2, axis=-1, keepdims=True)
    i2 = jnp.min(jnp.where(el2 == m2, lane, LANES), axis=-1, keepdims=True)
    e2 = jnp.exp(m2 - m1)
    w1 = p_g / (1.0 + e2)
    w2 = p_g * e2 / (1.0 + e2)
    ew = jnp.where(lane == i1, w1, jnp.where(lane == i2, w2, 0.0))

    acc = jnp.zeros(o_ref.shape, F32)
    for e in range(EXPERTS_PER_GROUP):
        a = _dot(x_hi, wg_ref[e])
        u = _dot(x_hi, wu_ref[e])
        hid = _silu(a) * u * ew[:, e:e + 1]
        acc = acc + _dot(hid.astype(BF16), wd_ref[e])
    o_ref[...] = acc


def _moe(tile_group, xs, wr_hi, wr_lo, br, wg, wu, wd, tmf):
    p, d = xs.shape
    nt = p // tmf
    grp3 = lambda a: pl.BlockSpec((None,) + a.shape[1:], lambda t, tg: (tg[t], 0, 0))
    grp4 = lambda a: pl.BlockSpec((None,) + a.shape[1:], lambda t, tg: (tg[t], 0, 0, 0))
    return pl.pallas_call(
        _moe_kernel,
        out_shape=jax.ShapeDtypeStruct((p, d), F32),
        grid_spec=pltpu.PrefetchScalarGridSpec(
            num_scalar_prefetch=1,
            grid=(nt,),
            in_specs=[pl.BlockSpec((tmf, d), lambda t, tg: (t, 0)),
                      grp3(wr_hi), grp3(wr_lo), grp3(br), grp4(wg), grp4(wu), grp4(wd)],
            out_specs=pl.BlockSpec((tmf, d), lambda t, tg: (t, 0))),
        compiler_params=_params(("arbitrary",)),
        name="moe",
    )(tile_group, xs, wr_hi, wr_lo, br, wg, wu, wd)


def _tiles(s):
    return dict(tm=min(512, s), tq=min(512, s), tk=min(512, s), lc=min(256, s),
                tmf=min(256, s))


def _encode_all(x, c, norm1_g, norm2_g, w_ada, b_ada, w_in, lam_q1, lam_k1, lam_q2, lam_k2,
                diff_norm_g, gla_wa_f, gla_ba_f, gla_wa_b, gla_ba_b, gla_norm_g, w_out,
                w_route_g, b_route_g, w_route_e, b_route_e, w_exp_gate, w_exp_up, w_exp_down,
                final_norm_g, w_ada_f, b_ada_f, tiles=None):
    nb, s, d = x.shape
    t = nb * s
    tl = tiles or _tiles(s)
    r = GLA_GATE_RANK

    c8 = jnp.zeros((8, d), F32).at[:nb].set(c)
    mod = _ada(c8, w_ada[0], b_ada[0])[:nb]
    mod_f = _ada(c8, w_ada_f, b_ada_f)[:nb]
    sh1, sc1, g1, sh2, sc2, g2 = [m.reshape(nb, 1, d) for m in jnp.split(mod, N_MOD, axis=-1)]
    sh_f, sc_f = [m.reshape(nb, 1, d) for m in jnp.split(mod_f, 2, axis=-1)]

    w_in0 = w_in[0]
    w_main = w_in0[:, :_C_MAIN].astype(BF16)
    w_z = jnp.zeros((d, LANES), F32).at[:, :2 * r].set(w_in0[:, _C_MAIN:]).astype(BF16)
    kw = N_HEADS_GLA * GLA_K_DIM
    w_gate = jnp.zeros((LANES, 2 * kw), F32)
    w_gate = w_gate.at[:r, :kw].set(gla_wa_f[0]).at[r:2 * r, kw:].set(gla_wa_b[0])
    wg_hi, wg_lo = _split2(w_gate)
    b_gate = jnp.concatenate([gla_ba_f[0], gla_ba_b[0]]).reshape(1, 2 * kw)

    qd, kd, vd, gq, gk, gv, gg, la = _pre(
        x, norm1_g[0].reshape(1, d), sh1, sc1, w_main, w_z, wg_hi, wg_lo, b_gate, tl["tm"])

    lamv = jnp.zeros((8, LANES), F32)
    for i, v in enumerate((lam_q1, lam_k1, lam_q2, lam_k2)):
        lamv = lamv.at[i, :DIFF_QK_DIM].set(v[0])
    od = _attn(lamv, diff_norm_g[0].reshape(1, DIFF_V_DIM), qd, kd, vd, tl["tq"], tl["tk"])
    of, ob = _gla(gq, gk, gv, la, tl["lc"])

    wr = jnp.zeros((d, LANES), F32).at[:, :N_GROUPS].set(w_route_g[0])
    wr_hi, wr_lo = _split2(wr)
    br = jnp.full((1, LANES), -1e30, F32).at[0, :N_GROUPS].set(b_route_g[0])
    x1, h2, route, cnt = _post(
        x, od, of, ob, gg, gla_norm_g[0].reshape(1, GLA_V_DIM), w_out[0].astype(BF16), g1,
        norm2_g[0].reshape(1, d), sh2, sc2, wr_hi, wr_lo, br, tl["tm"])

    tmf = tl["tmf"]
    route = route.reshape(t, LANES)
    g_idx = route[:, 0].astype(jnp.int32)
    rank = route[:, 1].astype(jnp.int32)
    counts = cnt[0, :N_GROUPS].astype(jnp.int32)
    padded = ((counts + tmf - 1) // tmf) * tmf
    ends = jnp.cumsum(padded)
    starts = ends - padded
    pos = starts[g_idx] + rank
    n_tiles = t // tmf + N_GROUPS
    tile_row = jnp.arange(n_tiles, dtype=jnp.int32) * tmf
    tile_group = jnp.minimum(jnp.sum(tile_row[:, None] >= ends[None, :], axis=1),
                             N_GROUPS - 1).astype(jnp.int32)
    tm = tl["tm"]
    pos3 = pos.reshape(t // tm, 1, tm)

    xs = _scatter(pos3, h2.reshape(t, d), n_tiles * tmf, tm)

    wre = w_route_e[0].reshape(d, N_GROUPS, EXPERTS_PER_GROUP).transpose(1, 0, 2)
    wrm = jnp.zeros((N_GROUPS, d, LANES), F32).at[:, :, :EXPERTS_PER_GROUP].set(wre)
    wrm = wrm.at[:, :, _GL0:_GL0 + N_GROUPS].set(jnp.broadcast_to(w_route_g[0], (N_GROUPS, d, N_GROUPS)))
    wrm_hi, wrm_lo = _split2(wrm)
    brm = jnp.zeros((N_GROUPS, 1, LANES), F32)
    brm = brm.at[:, 0, :EXPERTS_PER_GROUP].set(b_route_e[0].reshape(N_GROUPS, EXPERTS_PER_GROUP))
    brm = brm.at[:, 0, _GL0:_GL0 + N_GROUPS].set(jnp.broadcast_to(b_route_g[0], (N_GROUPS, N_GROUPS)))
    ys = _moe(tile_group, xs, wrm_hi, wrm_lo, brm, w_exp_gate[0].astype(BF16),
              w_exp_up[0].astype(BF16), w_exp_down[0].astype(BF16), tmf)

    return _final(pos3, ys, x1, g2, final_norm_g.reshape(1, d), sh_f, sc_f, tm)


def kernel(x_prompt, x_sample, c_prompt, c_sample, norm1_g, norm2_g, w_ada, b_ada, w_in, lam_q1, lam_k1, lam_q2, lam_k2, diff_norm_g, gla_wa_f, gla_ba_f, gla_wa_b, gla_ba_b, gla_norm_g, w_out, w_route_g, b_route_g, w_route_e, b_route_e, w_exp_gate, w_exp_up, w_exp_down, final_norm_g, w_ada_f, b_ada_f):
    nbp = x_prompt.shape[0]
    x = jnp.concatenate([x_prompt, x_sample], axis=0)
    c = jnp.concatenate([c_prompt, c_sample], axis=0)
    y = _encode_all(x, c, norm1_g, norm2_g, w_ada, b_ada, w_in, lam_q1, lam_k1, lam_q2, lam_k2,
                    diff_norm_g, gla_wa_f, gla_ba_f, gla_wa_b, gla_ba_b, gla_norm_g, w_out,
                    w_route_g, b_route_g, w_route_e, b_route_e, w_exp_gate, w_exp_up, w_exp_down,
                    final_norm_g, w_ada_f, b_ada_f)
    return (y[:nbp], y[nbp:])
```

```python
import functools
import math

import numpy as np
import jax
import jax.numpy as jnp
from jax import lax
from jax.experimental import pallas as pl
from jax.experimental.pallas import tpu as pltpu

F32 = jnp.float32
BF16 = jnp.bfloat16

D_MODEL = 1024
N_HEADS_DIFF = 4
DIFF_QK_DIM = 64
DIFF_V_DIM = 128
ROPE_DIM = 16
ROPE_THETA = 500000.0
N_HEADS_GLA = 4
GLA_K_DIM = 64
GLA_V_DIM = 128
GLA_GATE_RANK = 16
GLA_GATE_TAU = 16.0
GLA_CHUNK = 64
GLA_SUB = 16
N_GROUPS = 4
EXPERTS_PER_GROUP = 8
D_EXPERT = 256
NORM_EPS = 1e-6
N_MOD = 6
LAMBDA_INIT = 0.8 - 0.6 * math.exp(-0.3 * 0)
LOG2E = 1.4426950408889634

LANES = 128
VMEM_LIMIT = 56 * 1024 * 1024

_C_DQ, _C_DK, _C_DV = 0, 512, 1024
_C_GQ, _C_GK, _C_GV, _C_GG, _C_Z = 1536, 1792, 2048, 2560, 3072
_C_MAIN = _C_Z


def _dot(a, b):
    return jnp.dot(a, b, preferred_element_type=F32)


def _dot_nt(a, b):
    return lax.dot_general(a, b, (((1,), (1,)), ((), ())), preferred_element_type=F32)


def _dot_tn(a, b):
    return lax.dot_general(a, b, (((0,), (0,)), ((), ())), preferred_element_type=F32)


def _split2(x):
    hi = x.astype(BF16)
    lo = (x - hi.astype(F32)).astype(BF16)
    return hi, lo


def _dot3(a_hi, a_lo, b_hi, b_lo):
    return _dot(a_hi, b_hi) + (_dot(a_lo, b_hi) + _dot(a_hi, b_lo))


def _silu(x):
    return x / (1.0 + jnp.exp(-x))


def _log_sigmoid(x):
    return jnp.minimum(x, 0.0) - jnp.log(1.0 + jnp.exp(-jnp.abs(x)))


def _rms(x, g):
    ms = jnp.mean(x * x, axis=-1, keepdims=True)
    return x * lax.rsqrt(ms + NORM_EPS) * g


def _lane_tile(x, n):
    return jnp.concatenate([x] * n, axis=1)


def _params(sem, vmem=VMEM_LIMIT):
    return pltpu.CompilerParams(dimension_semantics=sem, vmem_limit_bytes=vmem)


def _ada_kernel(c_ref, w_ref, b_ref, o_ref):
    a_hi, a_lo = _split2(_silu(c_ref[...]))
    w_hi, w_lo = _split2(w_ref[...])
    o_ref[...] = _dot3(a_hi, a_lo, w_hi, w_lo) + b_ref[...]


def _ada(c8, w, b):
    d, n = w.shape
    tn = 1024
    return pl.pallas_call(
        _ada_kernel,
        out_shape=jax.ShapeDtypeStruct((c8.shape[0], n), F32),
        grid=(n // tn,),
        in_specs=[pl.BlockSpec((c8.shape[0], d), lambda j: (0, 0)),
                  pl.BlockSpec((d, tn), lambda j: (0, j)),
                  pl.BlockSpec((1, tn), lambda j: (0, j))],
        out_specs=pl.BlockSpec((c8.shape[0], tn), lambda j: (0, j)),
        compiler_params=_params(("arbitrary",)),
        name="ada",
    )(c8, w, b.reshape(1, n))


def _pre_kernel(x_ref, g_ref, sh_ref, sc_ref, wm_ref, wz_ref, wgh_ref, wgl_ref, bg_ref,
                rc_ref, rs1_ref, rs2_ref,
                qd_ref, kd_ref, vd_ref, gq_ref, gk_ref, gv_ref, gg_ref, la_ref):
    x = x_ref[...]
    h = _rms(x, g_ref[...]) * (1.0 + sc_ref[...]) + sh_ref[...]
    hb = h.astype(BF16)
    proj = _dot(hb, wm_ref[...])
    z = _dot(hb, wz_ref[...])

    rc, rs1, rs2 = rc_ref[...], rs1_ref[...], rs2_ref[...]

    def rope(t):
        return (t * rc + pltpu.roll(t, LANES - ROPE_DIM // 2, 1) * rs1
                + pltpu.roll(t, ROPE_DIM // 2, 1) * rs2)

    q_scale = (DIFF_QK_DIM ** -0.5) * LOG2E
    for hd in range(N_HEADS_DIFF):
        lo = hd * LANES
        qd_ref[:, lo:lo + LANES] = (rope(proj[:, _C_DQ + lo:_C_DQ + lo + LANES]) * q_scale).astype(BF16)
        kd_ref[:, lo:lo + LANES] = rope(proj[:, _C_DK + lo:_C_DK + lo + LANES]).astype(BF16)
    vd_ref[...] = proj[:, _C_DV:_C_GQ].astype(BF16)
    gq_ref[...] = (proj[:, _C_GQ:_C_GK] * (GLA_K_DIM ** -0.5)).astype(BF16)
    gk_ref[...] = proj[:, _C_GK:_C_GV].astype(BF16)
    gv_ref[...] = proj[:, _C_GV:_C_GG].astype(BF16)
    gg_ref[...] = proj[:, _C_GG:_C_Z].astype(BF16)

    z_hi, z_lo = _split2(z)
    zl = _dot3(z_hi, z_lo, wgh_ref[...], wgl_ref[...]) + bg_ref[...]
    la_ref[...] = _log_sigmoid(zl) * (1.0 / GLA_GATE_TAU)


def _rope_tables(seq_len):
    half = ROPE_DIM // 2
    inv_freq = ROPE_THETA ** (-jnp.arange(0, ROPE_DIM, 2, dtype=F32) / ROPE_DIM)
    ang = jnp.arange(seq_len, dtype=F32)[:, None] * inv_freq[None, :]
    cos, sin = jnp.cos(ang), jnp.sin(ang)
    ones = jnp.ones((seq_len, DIFF_QK_DIM - ROPE_DIM), F32)
    zeros = jnp.zeros((seq_len, DIFF_QK_DIM - ROPE_DIM), F32)
    zh = jnp.zeros((seq_len, half), F32)
    c64 = jnp.concatenate([cos, cos, ones], axis=1)
    s1_64 = jnp.concatenate([-sin, zh, zeros], axis=1)
    s2_64 = jnp.concatenate([zh, sin, zeros], axis=1)
    rep = lambda t: jnp.concatenate([t, t], axis=1)
    return rep(c64), rep(s1_64), rep(s2_64)


def _pre(x, g1, sh1, sc1, w_main, w_z, wg_hi, wg_lo, b_gate, tm):
    nb, s, d = x.shape
    rc, rs1, rs2 = _rope_tables(s)
    row = lambda w: pl.BlockSpec((None, tm, w), lambda b, i: (b, i, 0))
    full = lambda a: pl.BlockSpec(a.shape, lambda b, i: (0,) * a.ndim)
    per_seq = pl.BlockSpec((None, 1, d), lambda b, i: (b, 0, 0))
    tab = pl.BlockSpec((tm, LANES), lambda b, i: (i, 0))
    out_w = (512, 512, 512, 256, 256, 512, 512, 512)
    out_dt = (BF16,) * 7 + (F32,)
    return pl.pallas_call(
        _pre_kernel,
        out_shape=[jax.ShapeDtypeStruct((nb, s, w), dt) for w, dt in zip(out_w, out_dt)],
        grid=(nb, s // tm),
        in_specs=[row(d), full(g1), per_seq, per_seq, full(w_main), full(w_z), full(wg_hi),
                  full(wg_lo), full(b_gate), tab, tab, tab],
        out_specs=[row(w) for w in out_w],
        compiler_params=_params(("arbitrary", "arbitrary")),
        name="pre",
    )(x, g1, sh1, sc1, w_main, w_z, wg_hi, wg_lo, b_gate, rc, rs1, rs2)


def _attn_kernel(lam_ref, gn_ref, q_ref, k_ref, v_ref, o_ref,
                 vp_scr, s_scr, p_scr, al_scr, acc_scr, m_scr, *, tk):
    tq = q_ref.shape[0]
    s_len = k_ref.shape[0]
    n_blk = s_len // tk

    @pl.when(pl.program_id(2) == 0)
    def _():
        vp_scr[:, :LANES] = v_ref[...]
        vp_scr[:, LANES:] = jnp.ones((s_len, LANES), BF16)

    q = q_ref[...]
    lane = lax.broadcasted_iota(jnp.int32, q.shape, 1)
    zero = jnp.zeros_like(q)
    qcat = jnp.concatenate([jnp.where(lane < DIFF_QK_DIM, q, zero),
                            jnp.where(lane >= DIFF_QK_DIM, q, zero)], axis=0)

    def scores(blk, slot):
        off = pl.multiple_of(blk * tk, tk)
        s_scr[slot] = _dot_nt(qcat, k_ref[pl.ds(off, tk), :])

    def softmax(slot):
        s = s_scr[slot]
        m_prev = m_scr[...]
        m_new = jnp.maximum(m_prev, jnp.max(s, axis=1, keepdims=True))
        p_scr[slot] = jnp.exp2(s - _lane_tile(m_new, tk // LANES)).astype(BF16)
        al_scr[slot] = jnp.exp2(m_prev - m_new)
        m_scr[...] = m_new

    def values(blk, slot):
        off = pl.multiple_of(blk * tk, tk)
        pv = _dot(p_scr[slot], vp_scr[pl.ds(off, tk), :])
        acc_scr[...] = acc_scr[...] * _lane_tile(al_scr[slot], 2) + pv

    acc_scr[...] = jnp.zeros_like(acc_scr)
    m_scr[...] = jnp.full_like(m_scr, -jnp.inf)
    p_scr[1] = jnp.zeros(p_scr.shape[1:], BF16)
    al_scr[1] = jnp.ones(al_scr.shape[1:], F32)
    scores(0, 0)

    def body(i, carry):
        b0 = 2 * i
        scores(b0 + 1, 1)
        softmax(0)
        values(jnp.maximum(b0 - 1, 0), 1)
        scores(jnp.minimum(b0 + 2, n_blk - 1), 0)
        softmax(1)
        values(b0, 0)
        return carry

    lax.fori_loop(0, n_blk // 2, body, 0)
    values(n_blk - 1, 1)

    lv = lam_ref[...]
    lam = (jnp.exp(jnp.sum(lv[0:1] * lv[1:2], axis=-1, keepdims=True))
           - jnp.exp(jnp.sum(lv[2:3] * lv[3:4], axis=-1, keepdims=True)) + LAMBDA_INIT)
    a0 = acc_scr[0:tq, :]
    a1 = acc_scr[tq:2 * tq, :]
    o = a0[:, :LANES] / a0[:, LANES:] - lam * (a1[:, :LANES] / a1[:, LANES:])
    o_ref[...] = (_rms(o, gn_ref[...]) * (1.0 - LAMBDA_INIT)).astype(BF16)


def _attn(lamv, gn, qd, kd, vd, tq, tk):
    nb, s, _ = qd.shape
    return pl.pallas_call(
        functools.partial(_attn_kernel, tk=tk),
        out_shape=jax.ShapeDtypeStruct((nb, s, N_HEADS_DIFF * DIFF_V_DIM), BF16),
        grid=(nb, N_HEADS_DIFF, s // tq),
        in_specs=[pl.BlockSpec(lamv.shape, lambda b, h, i: (0, 0)),
                  pl.BlockSpec(gn.shape, lambda b, h, i: (0, 0)),
                  pl.BlockSpec((None, tq, LANES), lambda b, h, i: (b, i, h)),
                  pl.BlockSpec((None, s, LANES), lambda b, h, i: (b, 0, h)),
                  pl.BlockSpec((None, s, LANES), lambda b, h, i: (b, 0, h))],
        out_specs=pl.BlockSpec((None, tq, LANES), lambda b, h, i: (b, i, h)),
        scratch_shapes=[pltpu.VMEM((s, 2 * LANES), BF16),
                        pltpu.VMEM((2, 2 * tq, tk), F32),
                        pltpu.VMEM((2, 2 * tq, tk), BF16),
                        pltpu.VMEM((2, 2 * tq, LANES), F32),
                        pltpu.VMEM((2 * tq, 2 * LANES), F32),
                        pltpu.VMEM((2 * tq, LANES), F32)],
        compiler_params=_params(("arbitrary", "arbitrary", "arbitrary")),
        name="attn",
    )(lamv, gn, qd, kd, vd)


_C = GLA_CHUNK
_NSUB = GLA_CHUNK // GLA_SUB
_G_ROWS = (3 + _NSUB) * _C


def _gla_constants(reverse):
    i = np.arange(_C)[:, None]
    u = np.arange(_C)[None, :]
    if not reverse:
        cum = u <= i
        r_blk = [np.broadcast_to(u < GLA_SUB * I, (_C, _C)) for I in range(_NSUB)]
        rq = u < GLA_SUB * (i // GLA_SUB)
        causal = u <= i
        valid = [np.broadcast_to(i < GLA_SUB * (I + 1), (_C, LANES)) for I in range(_NSUB)]
    else:
        cum = u >= i
        r_blk = [np.broadcast_to(u >= GLA_SUB * (I + 1), (_C, _C)) for I in range(_NSUB)]
        rq = u >= GLA_SUB * (i // GLA_SUB + 1)
        causal = u >= i
        valid = [np.broadcast_to(i >= GLA_SUB * I, (_C, LANES)) for I in range(_NSUB)]
    cmat = np.concatenate([cum] + r_blk + [np.ones((_C, _C), bool), rq], axis=0)
    cm = np.zeros((2 * _C, LANES), np.float32)
    cm[:_C, :_C] = causal
    cm[_C:, :_C] = causal
    kvalid = np.concatenate(valid, axis=1).astype(np.float32)
    lane = np.arange(LANES)[None, :]
    qsel = []
    for hd in range(2):
        head_lanes = (lane // GLA_K_DIM) == hd
        qsel.append(np.concatenate(
            [np.broadcast_to(((i // GLA_SUB) == I) & head_lanes, (_C, LANES)) for I in range(_NSUB)],
            axis=1))
    qsel = np.concatenate(qsel, axis=0).astype(np.float32)
    return (jnp.asarray(cmat, BF16), jnp.asarray(cm), jnp.asarray(kvalid), jnp.asarray(qsel))


def _gla_chunk(la, q, k, v, state, cmat, causal, kvalid, qsel):
    la_hi, la_lo = _split2(la)
    lacat = jnp.concatenate([la_hi, la_lo], axis=1)
    g = _dot(cmat, lacat)
    gs = g[:, :LANES] + g[:, LANES:]
    b = gs[0:_C]
    r_blk = [gs[(1 + I) * _C:(2 + I) * _C] for I in range(_NSUB)]
    tot = gs[(1 + _NSUB) * _C:(2 + _NSUB) * _C]
    rq = gs[(2 + _NSUB) * _C:(3 + _NSUB) * _C]

    qf = q.astype(F32)
    kf = k.astype(F32)
    q_sub = qf * jnp.exp(b - rq)
    q_st = qf * jnp.exp(b)
    k_st = (kf * jnp.exp(tot - b)).astype(BF16)
    zero = jnp.zeros_like(kf)
    khat = jnp.concatenate(
        [jnp.where(kvalid[:, I * LANES:(I + 1) * LANES] > 0, kf * jnp.exp(r_blk[I] - b), zero)
         for I in range(_NSUB)], axis=1).astype(BF16)
    khat = jnp.concatenate([khat, jnp.zeros_like(khat)], axis=0)

    q_sub4 = jnp.concatenate([q_sub] * _NSUB, axis=1)
    q_sub8 = jnp.concatenate([q_sub4, q_sub4], axis=0)
    qhat = jnp.where(qsel > 0, q_sub8, jnp.zeros_like(q_sub8)).astype(BF16)
    att = _dot_nt(qhat, khat)
    att = jnp.where(causal > 0, att, jnp.zeros_like(att))

    lane = lax.broadcasted_iota(jnp.int32, q_st.shape, 1)
    zq = jnp.zeros_like(q_st)
    q_st2 = jnp.concatenate([jnp.where(lane < GLA_K_DIM, q_st, zq),
                             jnp.where(lane >= GLA_K_DIM, q_st, zq)], axis=0)
    lhs = jnp.concatenate([q_st2, att], axis=1).astype(BF16)
    rhs = jnp.concatenate([state.astype(BF16), v, jnp.zeros_like(v)], axis=0)
    o_all = _dot(lhs, rhs)
    o = jnp.concatenate([o_all[0:_C, 0:LANES], o_all[_C:2 * _C, LANES:2 * LANES]], axis=1)

    kv = _dot_tn(k_st, v)
    dcol = _dot_tn(lacat, jnp.ones((_C, LANES), BF16))
    decay = jnp.exp(dcol[:LANES] + dcol[LANES:])
    new_state = state * jnp.concatenate([decay, decay], axis=1) + kv
    return o, new_state


def _gla_kernel(cf_ref, mf_ref, kvf_ref, qsf_ref, cb_ref, mb_ref, kvb_ref, qsb_ref,
                qf_ref, kf_ref, vf_ref, laf_ref, qb_ref, kb_ref, vb_ref, lab_ref,
                of_ref, ob_ref, st_scr):
    @pl.when(pl.program_id(1) == 0)
    def _():
        st_scr[...] = jnp.zeros_like(st_scr)

    n_chunks = qf_ref.shape[0] // _C
    consts = ((cf_ref[...], mf_ref[...], kvf_ref[...], qsf_ref[...]),
              (cb_ref[...], mb_ref[...], kvb_ref[...], qsb_ref[...]))
    refs = ((qf_ref, kf_ref, vf_ref, laf_ref, of_ref), (qb_ref, kb_ref, vb_ref, lab_ref, ob_ref))
    for c in range(n_chunks):
        for d in range(2):
            q_ref, k_ref, v_ref, la_ref, o_ref = refs[d]
            cc = c if d == 0 else n_chunks - 1 - c
            rows = slice(cc * _C, (cc + 1) * _C)
            for pr in range(N_HEADS_GLA // 2):
                kl = slice(pr * LANES, (pr + 1) * LANES)
                vl = slice(pr * 2 * LANES, (pr + 1) * 2 * LANES)
                o, st = _gla_chunk(la_ref[rows, kl], q_ref[rows, kl], k_ref[rows, kl],
                                   v_ref[rows, vl], st_scr[d, pr], *consts[d])
                st_scr[d, pr] = st
                o_ref[rows, vl] = o


def _gla(gq, gk, gv, la, lc):
    nb, s, _ = gq.shape
    nblk = s // lc
    cf = _gla_constants(False)
    cb = _gla_constants(True)
    full = lambda a: pl.BlockSpec(a.shape, lambda b, i: (0,) * a.ndim)
    fwd = lambda w, col=0: pl.BlockSpec((None, lc, w), lambda b, i: (b, i, col))
    bwd = lambda w, col=0: pl.BlockSpec((None, lc, w), lambda b, i: (b, nblk - 1 - i, col))
    kw, vw = N_HEADS_GLA * GLA_K_DIM, N_HEADS_GLA * GLA_V_DIM
    return pl.pallas_call(
        _gla_kernel,
        out_shape=[jax.ShapeDtypeStruct((nb, s, vw), F32)] * 2,
        grid=(nb, nblk),
        in_specs=[full(a) for a in cf] + [full(a) for a in cb]
        + [fwd(kw), fwd(kw), fwd(vw), fwd(kw, 0), bwd(kw), bwd(kw), bwd(vw), bwd(kw, 1)],
        out_specs=[fwd(vw), bwd(vw)],
        scratch_shapes=[pltpu.VMEM((2, N_HEADS_GLA // 2, LANES, 2 * LANES), F32)],
        compiler_params=_params(("arbitrary", "arbitrary")),
        name="gla",
    )(*cf, *cb, gq, gk, gv, la, gq, gk, gv, la)


def _post_kernel(x_ref, od_ref, of_ref, ob_ref, gg_ref, gng_ref, wo_ref, g1_ref, n2_ref,
                 sh2_ref, sc2_ref, wrh_ref, wrl_ref, br_ref,
                 x1_ref, h2_ref, route_ref, cnt_ref, run_scr):
    first = jnp.logical_and(pl.program_id(0) == 0, pl.program_id(1) == 0)

    @pl.when(first)
    def _():
        run_scr[...] = jnp.zeros_like(run_scr)

    tm = x_ref.shape[0]
    og = of_ref[...] + ob_ref[...]
    gng = gng_ref[...]
    half = N_HEADS_DIFF * DIFF_V_DIM
    mix = _dot(od_ref[...], wo_ref[0:half, :])
    parts = []
    for hd in range(N_HEADS_GLA):
        sl = slice(hd * LANES, (hd + 1) * LANES)
        parts.append((_rms(og[:, sl], gng) * _silu(gg_ref[:, sl].astype(F32))).astype(BF16))
    mix = mix + _dot(jnp.concatenate(parts, axis=1), wo_ref[half:, :])
    x1 = x_ref[...] + g1_ref[...] * mix
    x1_ref[...] = x1
    h2 = _rms(x1, n2_ref[...]) * (1.0 + sc2_ref[...]) + sh2_ref[...]
    h2_ref[...] = h2

    h_hi, h_lo = _split2(h2)
    logits = _dot3(h_hi, h_lo, wrh_ref[...], wrl_ref[...]) + br_ref[...]
    lane = lax.broadcasted_iota(jnp.int32, logits.shape, 1)
    mx = jnp.max(logits, axis=-1, keepdims=True)
    idx = jnp.min(jnp.where(logits == mx, lane, LANES), axis=-1, keepdims=True)
    onehot = (lane == idx).astype(F32)
    r_i = lax.broadcasted_iota(jnp.int32, (tm, tm), 0)
    c_i = lax.broadcasted_iota(jnp.int32, (tm, tm), 1)
    before = (c_i < r_i).astype(BF16)
    prior = _dot(before, onehot.astype(BF16)) + run_scr[...]
    rank = jnp.sum(onehot * prior, axis=-1, keepdims=True)
    run_scr[...] = run_scr[...] + jnp.sum(onehot, axis=0, keepdims=True)
    route_ref[...] = jnp.where(lane == 0, idx.astype(F32), jnp.where(lane == 1, rank, 0.0))
    cnt_ref[...] = run_scr[...]


def _post(x, od, of, ob, gg, gng, wo, g1, n2, sh2, sc2, wr_hi, wr_lo, br, tm):
    nb, s, d = x.shape
    row = lambda w: pl.BlockSpec((None, tm, w), lambda b, i: (b, i, 0))
    full = lambda a: pl.BlockSpec(a.shape, lambda b, i: (0,) * a.ndim)
    per_seq = pl.BlockSpec((None, 1, d), lambda b, i: (b, 0, 0))
    return pl.pallas_call(
        _post_kernel,
        out_shape=[jax.ShapeDtypeStruct((nb, s, d), F32), jax.ShapeDtypeStruct((nb, s, d), F32),
                   jax.ShapeDtypeStruct((nb, s, LANES), F32), jax.ShapeDtypeStruct((1, LANES), F32)],
        grid=(nb, s // tm),
        in_specs=[row(d), row(512), row(512), row(512), row(512), full(gng), full(wo), per_seq,
                  full(n2), per_seq, per_seq, full(wr_hi), full(wr_lo), full(br)],
        out_specs=[row(d), row(d), row(LANES), pl.BlockSpec((1, LANES), lambda b, i: (0, 0))],
        scratch_shapes=[pltpu.VMEM((1, LANES), F32)],
        compiler_params=_params(("arbitrary", "arbitrary")),
        name="post",
    )(x, od, of, ob, gg, gng, wo, g1, n2, sh2, sc2, wr_hi, wr_lo, br)


def _row_copy(src_ref, src_row, dst_ref, dst_row, sem):
    return pltpu.make_async_copy(src_ref.at[pl.ds(src_row, 1)], dst_ref.at[pl.ds(dst_row, 1)], sem)


def _scatter_kernel(pos_ref, h_ref, init_ref, xs_ref, sem):
    del init_ref
    tm = h_ref.shape[0]

    def start(r, c):
        _row_copy(h_ref, r, xs_ref, pos_ref[0, r], sem).start()
        return c

    def wait(r, c):
        _row_copy(h_ref, 0, xs_ref, 0, sem).wait()
        return c

    lax.fori_loop(0, tm, start, 0)
    lax.fori_loop(0, tm, wait, 0)


def _scatter(pos3, h2, n_rows, tm):
    t, d = h2.shape
    init = jnp.zeros((n_rows, d), F32)
    return pl.pallas_call(
        _scatter_kernel,
        out_shape=jax.ShapeDtypeStruct((n_rows, d), F32),
        grid=(t // tm,),
        in_specs=[pl.BlockSpec((None, 1, tm), lambda i: (i, 0, 0), memory_space=pltpu.SMEM),
                  pl.BlockSpec((tm, d), lambda i: (i, 0)),
                  pl.BlockSpec(memory_space=pl.ANY)],
        out_specs=pl.BlockSpec(memory_space=pl.ANY),
        scratch_shapes=[pltpu.SemaphoreType.DMA(())],
        input_output_aliases={2: 0},
        compiler_params=_params(("arbitrary",)),
        name="scatter",
    )(pos3, h2, init)


def _final_kernel(pos_ref, ys_ref, x1_ref, g2_ref, fn_ref, shf_ref, scf_ref, o_ref, buf, sem):
    tm = x1_ref.shape[0]

    def start(r, c):
        _row_copy(ys_ref, pos_ref[0, r], buf, r, sem).start()
        return c

    def wait(r, c):
        _row_copy(ys_ref, 0, buf, 0, sem).wait()
        return c

    lax.fori_loop(0, tm, start, 0)
    lax.fori_loop(0, tm, wait, 0)
    y = x1_ref[...] + g2_ref[...] * buf[...]
    o_ref[...] = _rms(y, fn_ref[...]) * (1.0 + scf_ref[...]) + shf_ref[...]


def _final(pos3, ys, x1, g2, fn, shf, scf, tm):
    nb, s, d = x1.shape
    nt = s // tm
    row = pl.BlockSpec((None, tm, d), lambda b, i: (b, i, 0))
    per_seq = pl.BlockSpec((None, 1, d), lambda b, i: (b, 0, 0))
    return pl.pallas_call(
        _final_kernel,
        out_shape=jax.ShapeDtypeStruct((nb, s, d), F32),
        grid=(nb, nt),
        in_specs=[pl.BlockSpec((None, 1, tm), lambda b, i: (b * nt + i, 0, 0), memory_space=pltpu.SMEM),
                  pl.BlockSpec(memory_space=pl.ANY), row, per_seq,
                  pl.BlockSpec(fn.shape, lambda b, i: (0, 0)), per_seq, per_seq],
        out_specs=row,
        scratch_shapes=[pltpu.VMEM((tm, d), F32), pltpu.SemaphoreType.DMA(())],
        compiler_params=_params(("arbitrary", "arbitrary")),
        name="final",
    )(pos3, ys, x1, g2, fn, shf, scf)


_GL0 = EXPERTS_PER_GROUP


def _moe_kernel(tg_ref, x_ref, wrh_ref, wrl_ref, br_ref, wg_ref, wu_ref, wd_ref, o_ref):
    grp = tg_ref[pl.program_id(0)]
    x = x_ref[...]
    x_hi, x_lo = _split2(x)
    logits = _dot3(x_hi, x_lo, wrh_ref[...], wrl_ref[...]) + br_ref[...]
    lane = lax.broadcasted_iota(jnp.int32, logits.shape, 1)
    ninf = jnp.full_like(logits, -jnp.inf)

    gl = jnp.where(jnp.logical_and(lane >= _GL0, lane < _GL0 + N_GROUPS), logits, ninf)
    ge = jnp.exp(gl - jnp.max(gl, axis=-1, keepdims=True))
    p_g = (jnp.sum(jnp.where(lane == _GL0 + grp, ge, 0.0), axis=-1, keepdims=True)
           / jnp.sum(ge, axis=-1, keepdims=True))

    el = jnp.where(lane < EXPERTS_PER_GROUP, logits, ninf)
    m1 = jnp.max(el, axis=-1, keepdims=True)
    i1 = jnp.min(jnp.where(el == m1, lane, LANES), axis=-1, keepdims=True)
    el2 = jnp.where(lane == i1, ninf, el)
    m2 = jnp.max(el2, axis=-1, keepdims=True)
    i2 = jnp.min(jnp.where(el2 == m2, lane, LANES), axis=-1, keepdims=True)
    e2 = jnp.exp(m2 - m1)
    w1 = p_g / (1.0 + e2)
    w2 = p_g * e2 / (1.0 + e2)
    ew = jnp.where(lane == i1, w1, jnp.where(lane == i2, w2, 0.0))

    acc = jnp.zeros(o_ref.shape, F32)
    for e in range(EXPERTS_PER_GROUP):
        a = _dot(x_hi, wg_ref[e])
        u = _dot(x_hi, wu_ref[e])
        hid = _silu(a) * u * ew[:, e:e + 1]
        acc = acc + _dot(hid.astype(BF16), wd_ref[e])
    o_ref[...] = acc


def _moe(tile_group, xs, wr_hi, wr_lo, br, wg, wu, wd, tmf):
    p, d = xs.shape
    nt = p // tmf
    grp3 = lambda a: pl.BlockSpec((None,) + a.shape[1:], lambda t, tg: (tg[t], 0, 0))
    grp4 = lambda a: pl.BlockSpec((None,) + a.shape[1:], lambda t, tg: (tg[t], 0, 0, 0))
    return pl.pallas_call(
        _moe_kernel,
        out_shape=jax.ShapeDtypeStruct((p, d), F32),
        grid_spec=pltpu.PrefetchScalarGridSpec(
            num_scalar_prefetch=1,
            grid=(nt,),
            in_specs=[pl.BlockSpec((tmf, d), lambda t, tg: (t, 0)),
                      grp3(wr_hi), grp3(wr_lo), grp3(br), grp4(wg), grp4(wu), grp4(wd)],
            out_specs=pl.BlockSpec((tmf, d), lambda t, tg: (t, 0))),
        compiler_params=_params(("arbitrary",)),
        name="moe",
    )(tile_group, xs, wr_hi, wr_lo, br, wg, wu, wd)


def _tiles(s):
    return dict(tm=min(512, s), tq=min(512, s), tk=min(512, s), lc=min(256, s),
                tmf=min(256, s))


def _encode_all(x, c, norm1_g, norm2_g, w_ada, b_ada, w_in, lam_q1, lam_k1, lam_q2, lam_k2,
                diff_norm_g, gla_wa_f, gla_ba_f, gla_wa_b, gla_ba_b, gla_norm_g, w_out,
                w_route_g, b_route_g, w_route_e, b_route_e, w_exp_gate, w_exp_up, w_exp_down,
                final_norm_g, w_ada_f, b_ada_f, tiles=None):
    nb, s, d = x.shape
    t = nb * s
    tl = tiles or _tiles(s)
    r = GLA_GATE_RANK

    c8 = jnp.zeros((8, d), F32).at[:nb].set(c)
    mod = _ada(c8, w_ada[0], b_ada[0])[:nb]
    mod_f = _ada(c8, w_ada_f, b_ada_f)[:nb]
    sh1, sc1, g1, sh2, sc2, g2 = [m.reshape(nb, 1, d) for m in jnp.split(mod, N_MOD, axis=-1)]
    sh_f, sc_f = [m.reshape(nb, 1, d) for m in jnp.split(mod_f, 2, axis=-1)]

    w_in0 = w_in[0]
    w_main = w_in0[:, :_C_MAIN].astype(BF16)
    w_z = jnp.zeros((d, LANES), F32).at[:, :2 * r].set(w_in0[:, _C_MAIN:]).astype(BF16)
    kw = N_HEADS_GLA * GLA_K_DIM
    w_gate = jnp.zeros((LANES, 2 * kw), F32)
    w_gate = w_gate.at[:r, :kw].set(gla_wa_f[0]).at[r:2 * r, kw:].set(gla_wa_b[0])
    wg_hi, wg_lo = _split2(w_gate)
    b_gate = jnp.concatenate([gla_ba_f[0], gla_ba_b[0]]).reshape(1, 2 * kw)

    qd, kd, vd, gq, gk, gv, gg, la = _pre(
        x, norm1_g[0].reshape(1, d), sh1, sc1, w_main, w_z, wg_hi, wg_lo, b_gate, tl["tm"])

    lamv = jnp.zeros((8, LANES), F32)
    for i, v in enumerate((lam_q1, lam_k1, lam_q2, lam_k2)):
        lamv = lamv.at[i, :DIFF_QK_DIM].set(v[0])
    od = _attn(lamv, diff_norm_g[0].reshape(1, DIFF_V_DIM), qd, kd, vd, tl["tq"], tl["tk"])
    of, ob = _gla(gq, gk, gv, la, tl["lc"])

    wr = jnp.zeros((d, LANES), F32).at[:, :N_GROUPS].set(w_route_g[0])
    wr_hi, wr_lo = _split2(wr)
    br = jnp.full((1, LANES), -1e30, F32).at[0, :N_GROUPS].set(b_route_g[0])
    x1, h2, route, cnt = _post(
        x, od, of, ob, gg, gla_norm_g[0].reshape(1, GLA_V_DIM), w_out[0].astype(BF16), g1,
        norm2_g[0].reshape(1, d), sh2, sc2, wr_hi, wr_lo, br, tl["tm"])

    tmf = tl["tmf"]
    route = route.reshape(t, LANES)
    g_idx = route[:, 0].astype(jnp.int32)
    rank = route[:, 1].astype(jnp.int32)
    counts = cnt[0, :N_GROUPS].astype(jnp.int32)
    padded = ((counts + tmf - 1) // tmf) * tmf
    ends = jnp.cumsum(padded)
    starts = ends - padded
    pos = starts[g_idx] + rank
    n_tiles = t // tmf + N_GROUPS
    tile_row = jnp.arange(n_tiles, dtype=jnp.int32) * tmf
    tile_group = jnp.minimum(jnp.sum(tile_row[:, None] >= ends[None, :], axis=1),
                             N_GROUPS - 1).astype(jnp.int32)
    tm = tl["tm"]
    pos3 = pos.reshape(t // tm, 1, tm)

    xs = _scatter(pos3, h2.reshape(t, d), n_tiles * tmf, tm)

    wre = w_route_e[0].reshape(d, N_GROUPS, EXPERTS_PER_GROUP).transpose(1, 0, 2)
    wrm = jnp.zeros((N_GROUPS, d, LANES), F32).at[:, :, :EXPERTS_PER_GROUP].set(wre)
    wrm = wrm.at[:, :, _GL0:_GL0 + N_GROUPS].set(jnp.broadcast_to(w_route_g[0], (N_GROUPS, d, N_GROUPS)))
    wrm_hi, wrm_lo = _split2(wrm)
    brm = jnp.zeros((N_GROUPS, 1, LANES), F32)
    brm = brm.at[:, 0, :EXPERTS_PER_GROUP].set(b_route_e[0].reshape(N_GROUPS, EXPERTS_PER_GROUP))
    brm = brm.at[:, 0, _GL0:_GL0 + N_GROUPS].set(jnp.broadcast_to(b_route_g[0], (N_GROUPS, N_GROUPS)))
    ys = _moe(tile_group, xs, wrm_hi, wrm_lo, brm, w_exp_gate[0].astype(BF16),
              w_exp_up[0].astype(BF16), w_exp_down[0].astype(BF16), tmf)

    return _final(pos3, ys, x1, g2, final_norm_g.reshape(1, d), sh_f, sc_f, tm)


def kernel(x_prompt, x_sample, c_prompt, c_sample, norm1_g, norm2_g, w_ada, b_ada, w_in, lam_q1, lam_k1, lam_q2, lam_k2, diff_norm_g, gla_wa_f, gla_ba_f, gla_wa_b, gla_ba_b, gla_norm_g, w_out, w_route_g, b_route_g, w_route_e, b_route_e, w_exp_gate, w_exp_up, w_exp_down, final_norm_g, w_ada_f, b_ada_f):
    nbp = x_prompt.shape[0]
    x = jnp.concatenate([x_prompt, x_sample], axis=0)
    c = jnp.concatenate([c_prompt, c_sample], axis=0)
    y = _encode_all(x, c, norm1_g, norm2_g, w_ada, b_ada, w_in, lam_q1, lam_k1, lam_q2, lam_k2,
                    diff_norm_g, gla_wa_f, gla_ba_f, gla_wa_b, gla_ba_b, gla_norm_g, w_out,
                    w_route_g, b_route_g, w_route_e, b_route_e, w_exp_gate, w_exp_up, w_exp_down,
                    final_norm_g, w_ada_f, b_ada_f)
    return (y[:nbp], y[nbp:])
```
